```python
import jax, jax.numpy as jnp
from jax import lax
import numpy as np

D_MODEL = 2048
BATCH = 4
SEQ = 8192
DEPTH = 4
DEC_BATCH = 8
DEC_SEQ = 64
PAST_LEN = 2048

CHUNK = 64
N_META = 16
Q_BLOCK = 128
LEAD_PAD = Q_BLOCK - N_META
SB_BLK = 128
EXPAND = 1
D_MIX = EXPAND * D_MODEL
N_MIXERS = 3
N_A = (DEPTH + 2) // N_MIXERS
N_B = (DEPTH + 1) // N_MIXERS
N_C = DEPTH // N_MIXERS
A_DH = 128
A_HEADS = D_MIX // A_DH
A_COLS = 4 * D_MIX + A_HEADS
FORGET_BIAS = 3.0
B_DH = 64
B_HEADS = D_MIX // B_DH
B_KV = B_HEADS // 8
B_GROUP = B_HEADS // B_KV
B_COLS = 2 * D_MIX + 2 * B_KV * B_DH
WINDOW = 128
WIN_CHUNKS = WINDOW // CHUNK
SWA_ROWS = min(WINDOW, PAST_LEN)
C_DH = 128
C_HEADS = D_MIX // C_DH
C_COLS = 4 * D_MIX
EPS = 1e-6
NEG = -1e30

kernel_name = "hybrid_fox_swa_stickbreak_stream_step"


def rmsnorm(x, g):
    xf = x.astype(jnp.float32)
    y = xf * lax.rsqrt(jnp.mean(xf * xf, axis=-1, keepdims=True) + EPS) * g.astype(jnp.float32)
    return y.astype(x.dtype)


def split_cols(u, sizes):
    return jnp.split(u, [int(s) for s in np.cumsum(sizes)[:-1]], axis=-1)


def alibi_slopes(n):
    return jnp.asarray(2.0 ** (-8.0 * np.arange(1, n + 1) / n), jnp.float32)


def mixer_out(o, z, w_out):
    b, t = o.shape[:2]
    return (o.reshape(b, t, D_MIX) * jax.nn.silu(z)) @ w_out


def fox_project(h, w_in, b_f, qn_g, kn_g):
    b, t = h.shape[:2]
    q, k, v, z, fl = split_cols(h @ w_in, [D_MIX, D_MIX, D_MIX, D_MIX, A_HEADS])
    q = rmsnorm(q.reshape(b, t, A_HEADS, A_DH), qn_g)
    k = rmsnorm(k.reshape(b, t, A_HEADS, A_DH), kn_g)
    v = v.reshape(b, t, A_HEADS, A_DH)
    logf = jax.nn.log_sigmoid(fl.astype(jnp.float32) + b_f.astype(jnp.float32))
    return q, k, v, z, logf


def fox_core(q, k, v, fk, mask):
    s = jnp.einsum("bqhd,bkhd->bhqk", q, k, preferred_element_type=jnp.float32) * (A_DH ** -0.5)
    s = jnp.where(mask, s - jnp.swapaxes(fk, 1, 2)[..., None, :], NEG)
    p = jax.nn.softmax(s, axis=-1)
    return jnp.einsum("bhqk,bkhd->bqhd", p.astype(v.dtype), v)


def fox_prompt(q, k, v, logf):
    L = q.shape[1]
    F = jnp.cumsum(logf, axis=1)
    outs = []
    for i in range(L // Q_BLOCK):
        end = (i + 1) * Q_BLOCK
        qpos = i * Q_BLOCK + np.arange(Q_BLOCK)
        kpos = np.arange(end)
        mask = jnp.asarray((kpos[None, :] <= qpos[:, None]) & (kpos[None, :] >= LEAD_PAD))
        outs.append(fox_core(q[:, end - Q_BLOCK:end], k[:, :end], v[:, :end], F[:, :end], mask))
    return jnp.concatenate(outs, axis=1)


def fox_sample(q, k, v, logf, ck, cv, clogf):
    p_len, t = ck.shape[1], q.shape[1]
    kk = jnp.concatenate([ck.astype(k.dtype), k], axis=1)
    vv = jnp.concatenate([cv.astype(v.dtype), v], axis=1)
    F = jnp.cumsum(jnp.concatenate([clogf.astype(jnp.float32), logf], axis=1), axis=1)
    mask = jnp.arange(p_len + t)[None, :] <= (p_len + jnp.arange(t))[:, None]
    return fox_core(q, kk, vv, F, mask)


def swa_project(h, w_in, qn_g, kn_g):
    b, t = h.shape[:2]
    q, k, v, z = split_cols(h @ w_in, [D_MIX, B_KV * B_DH, B_KV * B_DH, D_MIX])
    q = rmsnorm(q.reshape(b, t, B_KV, B_GROUP, B_DH), qn_g)
    k = rmsnorm(k.reshape(b, t, B_KV, B_DH), kn_g)
    v = v.reshape(b, t, B_KV, B_DH)
    return q, k, v, z


def swa_core(q, k, v, dist, valid, sinks, slopes):
    b, tq = q.shape[:2]
    s = jnp.einsum("bqhgd,bkhd->bhgqk", q, k, preferred_element_type=jnp.float32) * (B_DH ** -0.5)
    s = jnp.where(valid, s - slopes.reshape(B_KV, B_GROUP, 1, 1) * dist, NEG)
    sink = jnp.broadcast_to(sinks.astype(jnp.float32).reshape(1, B_KV, B_GROUP, 1, 1), s.shape[:-1] + (1,))
    p = jax.nn.softmax(jnp.concatenate([s, sink], axis=-1), axis=-1)[..., :-1]
    o = jnp.einsum("bhgqk,bkhd->bqhgd", p.astype(v.dtype), v)
    return o.reshape(b, tq, B_HEADS, B_DH)


def swa_prompt(q, k, v, sinks, slopes):
    L = q.shape[1]
    band = (WIN_CHUNKS + 1) * CHUNK
    padw = ((0, 0), (WIN_CHUNKS * CHUNK, 0), (0, 0), (0, 0))
    kpad, vpad = jnp.pad(k, padw), jnp.pad(v, padw)
    kj = jnp.arange(band)
    dist = jnp.abs(WIN_CHUNKS * CHUNK + jnp.arange(CHUNK)[:, None] - kj[None, :]).astype(jnp.float32)

    def block(c):
        start = c * CHUNK
        qc = lax.dynamic_slice_in_dim(q, start, CHUNK, 1)
        kc = lax.dynamic_slice_in_dim(kpad, start, band, 1)
        vc = lax.dynamic_slice_in_dim(vpad, start, band, 1)
        valid = (start - WIN_CHUNKS * CHUNK + kj) >= LEAD_PAD
        return swa_core(qc, kc, vc, dist, valid[None, :], sinks, slopes)

    out = lax.map(block, jnp.arange(L // CHUNK))
    n, b, blk = out.shape[:3]
    return jnp.moveaxis(out, 0, 1).reshape((b, n * blk) + out.shape[3:])


def swa_sample(q, k, v, ck, cv, sinks, slopes):
    r, t = ck.shape[1], q.shape[1]
    kk = jnp.concatenate([ck.astype(k.dtype), k], axis=1)
    vv = jnp.concatenate([cv.astype(v.dtype), v], axis=1)
    dist = jnp.abs(r + jnp.arange(t)[:, None] - jnp.arange(r + t)[None, :]).astype(jnp.float32)
    valid = jnp.ones((t, r + t), bool)
    return swa_core(q, kk, vv, dist, valid, sinks, slopes)


def sb_project(h, w_in):
    b, t = h.shape[:2]
    q, k, v, z = split_cols(h @ w_in, [D_MIX, D_MIX, D_MIX, D_MIX])
    shp = (b, t, C_HEADS, C_DH)
    return q.reshape(shp), k.reshape(shp), v.reshape(shp), z


def rev_excl_cumsum(x):
    n = x.shape[-1]
    nb = -(-n // SB_BLK)
    xb = jnp.pad(x, [(0, 0)] * (x.ndim - 1) + [(0, nb * SB_BLK - n)]).reshape(x.shape[:-1] + (nb, SB_BLK))
    upper = jnp.asarray(np.tril(np.ones((SB_BLK, SB_BLK), np.float32), -1))
    within = jnp.einsum("...cj,js->...cs", xb, upper, precision=lax.Precision.HIGHEST)
    later_blocks = jnp.asarray(np.tril(np.ones((nb, nb), np.float32), -1))
    after = jnp.einsum("...e,ec->...c", jnp.sum(xb, axis=-1), later_blocks, precision=lax.Precision.HIGHEST)
    return (within + after[..., None]).reshape(x.shape[:-1] + (nb * SB_BLK,))[..., :n]


def sb_core(q, k, v, mask):
    z = jnp.einsum("bqhd,bkhd->bhqk", q, k, preferred_element_type=jnp.float32) * (C_DH ** -0.5)
    log_b = jax.nn.log_sigmoid(z)
    log_1mb = jnp.where(mask, log_b - z, 0.0)
    a = jnp.where(mask, jnp.exp(log_b + rev_excl_cumsum(log_1mb)), 0.0)
    return jnp.einsum("bhqk,bkhd->bqhd", a.astype(v.dtype), v)


def sb_prompt(q, k, v):
    L = q.shape[1]
    outs = []
    for i in range(L // Q_BLOCK):
        end = (i + 1) * Q_BLOCK
        qpos = i * Q_BLOCK + np.arange(Q_BLOCK)
        kpos = np.arange(end)
        mask = jnp.asarray((kpos[None, :] < qpos[:, None]) & (kpos[None, :] >= LEAD_PAD))
        outs.append(sb_core(q[:, end - Q_BLOCK:end], k[:, :end], v[:, :end], mask))
    return jnp.concatenate(outs, axis=1)


def sb_sample(q, k, v, ck, cv):
    p_len, t = ck.shape[1], q.shape[1]
    kk = jnp.concatenate([ck.astype(k.dtype), k], axis=1)
    vv = jnp.concatenate([cv.astype(v.dtype), v], axis=1)
    mask = jnp.arange(p_len + t)[None, :] < (p_len + jnp.arange(t))[:, None]
    return sb_core(q, kk, vv, mask)


def setup_inputs(seed: int = 0) -> dict:
    key = jax.random.key(seed)
    keys = iter(jax.random.split(key, 24))

    def nrm(shape, scale=1.0):
        return scale * jax.random.normal(next(keys), shape, jnp.float32)

    return {
        "x_prompt": nrm((BATCH, SEQ, D_MODEL)),
        "x_sample": nrm((DEC_BATCH, DEC_SEQ, D_MODEL)),
        "cache_k_a": nrm((N_A, DEC_BATCH, PAST_LEN, A_HEADS, A_DH)),
        "cache_v_a": nrm((N_A, DEC_BATCH, PAST_LEN, A_HEADS, A_DH)),
        "cache_logf_a": jax.nn.log_sigmoid(FORGET_BIAS + nrm((N_A, DEC_BATCH, PAST_LEN, A_HEADS))),
        "cache_k_b": nrm((N_B, DEC_BATCH, SWA_ROWS, B_KV, B_DH)),
        "cache_v_b": nrm((N_B, DEC_BATCH, SWA_ROWS, B_KV, B_DH)),
        "cache_k_c": nrm((N_C, DEC_BATCH, PAST_LEN, C_HEADS, C_DH)),
        "cache_v_c": nrm((N_C, DEC_BATCH, PAST_LEN, C_HEADS, C_DH)),
        "meta_tokens": nrm((N_META, D_MODEL)),
        "norm_g": 1.0 + nrm((DEPTH, D_MODEL), 0.02),
        "w_in_a": nrm((N_A, D_MODEL, A_COLS), D_MODEL ** -0.5),
        "w_out_a": nrm((N_A, D_MIX, D_MODEL), D_MIX ** -0.5),
        "b_f_a": FORGET_BIAS + nrm((N_A, A_HEADS), 0.5),
        "qn_g_a": 1.0 + nrm((N_A, A_DH), 0.02),
        "kn_g_a": 1.0 + nrm((N_A, A_DH), 0.02),
        "w_in_b": nrm((N_B, D_MODEL, B_COLS), D_MODEL ** -0.5),
        "w_out_b": nrm((N_B, D_MIX, D_MODEL), D_MIX ** -0.5),
        "sinks_b": nrm((N_B, B_HEADS), 0.5),
        "qn_g_b": 1.0 + nrm((N_B, B_DH), 0.02),
        "kn_g_b": 1.0 + nrm((N_B, B_DH), 0.02),
        "w_in_c": nrm((N_C, D_MODEL, C_COLS), D_MODEL ** -0.5),
        "w_out_c": nrm((N_C, D_MIX, D_MODEL), D_MIX ** -0.5),
    }


def reference(x_prompt, x_sample, cache_k_a, cache_v_a, cache_logf_a, cache_k_b, cache_v_b,
              cache_k_c, cache_v_c, meta_tokens, norm_g, w_in_a, w_out_a, b_f_a, qn_g_a, kn_g_a,
              w_in_b, w_out_b, sinks_b, qn_g_b, kn_g_b, w_in_c, w_out_c):
    b = x_prompt.shape[0]
    dt = x_prompt.dtype
    xp = jnp.concatenate([jnp.zeros((b, LEAD_PAD, D_MODEL), dt),
                          jnp.broadcast_to(meta_tokens.astype(dt), (b, N_META, D_MODEL)),
                          x_prompt], axis=1)
    xs = x_sample
    slopes_b = alibi_slopes(B_HEADS)
    ka_p, va_p, fa_p, ka_s, va_s, fa_s = [], [], [], [], [], []
    kb_p, vb_p, kb_s, vb_s = [], [], [], []
    kc_p, vc_p, kc_s, vc_s = [], [], [], []
    for i in range(DEPTH):
        j = i // N_MIXERS
        hp = rmsnorm(xp, norm_g[i])
        hs = rmsnorm(xs, norm_g[i])
        if i % N_MIXERS == 0:
            qp, kp, vp, zp, lfp = fox_project(hp, w_in_a[j], b_f_a[j], qn_g_a[j], kn_g_a[j])
            qs, ks, vs, zs, lfs = fox_project(hs, w_in_a[j], b_f_a[j], qn_g_a[j], kn_g_a[j])
            yp = mixer_out(fox_prompt(qp, kp, vp, lfp), zp, w_out_a[j])
            ys = mixer_out(fox_sample(qs, ks, vs, lfs, cache_k_a[j], cache_v_a[j], cache_logf_a[j]), zs, w_out_a[j])
            ka_p.append(kp[:, LEAD_PAD:]); va_p.append(vp[:, LEAD_PAD:]); fa_p.append(lfp[:, LEAD_PAD:])
            ka_s.append(ks); va_s.append(vs); fa_s.append(lfs)
        elif i % N_MIXERS == 1:
            qp, kp, vp, zp = swa_project(hp, w_in_b[j], qn_g_b[j], kn_g_b[j])
            qs, ks, vs, zs = swa_project(hs, w_in_b[j], qn_g_b[j], kn_g_b[j])
            yp = mixer_out(swa_prompt(qp, kp, vp, sinks_b[j], slopes_b), zp, w_out_b[j])
            ys = mixer_out(swa_sample(qs, ks, vs, cache_k_b[j], cache_v_b[j], sinks_b[j], slopes_b), zs, w_out_b[j])
            kb_p.append(kp[:, -SWA_ROWS:]); vb_p.append(vp[:, -SWA_ROWS:])
            kb_s.append(ks); vb_s.append(vs)
        else:
            qp, kp, vp, zp = sb_project(hp, w_in_c[j])
            qs, ks, vs, zs = sb_project(hs, w_in_c[j])
            yp = mixer_out(sb_prompt(qp, kp, vp), zp, w_out_c[j])
            ys = mixer_out(sb_sample(qs, ks, vs, cache_k_c[j], cache_v_c[j]), zs, w_out_c[j])
            kc_p.append(kp[:, LEAD_PAD:]); vc_p.append(vp[:, LEAD_PAD:])
            kc_s.append(ks); vc_s.append(vs)
        xp = xp + yp
        xs = xs + ys
    y_prompt = xp[:, Q_BLOCK:]
    y_sample = xs
    return (y_prompt, y_sample,
            jnp.stack(ka_p), jnp.stack(va_p), jnp.stack(fa_p),
            jnp.stack(ka_s), jnp.stack(va_s), jnp.stack(fa_s),
            jnp.stack(kb_p), jnp.stack(vb_p), jnp.stack(kb_s), jnp.stack(vb_s),
            jnp.stack(kc_p), jnp.stack(vc_p), jnp.stack(kc_s), jnp.stack(vc_s))
```

```python
import functools

import numpy as np
import jax
import jax.numpy as jnp
from jax import lax
from jax.experimental import pallas as pl
from jax.experimental.pallas import tpu as pltpu

CHUNK = 64
N_META = 16
Q_BLOCK = 128
LEAD_PAD = Q_BLOCK - N_META
N_MIXERS = 3
A_DH = 128
B_DH = 64
B_GROUP = 8
C_DH = 128
WINDOW = 128
FORGET_LANES = 128
EPS = 1e-6
NEG = -1e30

LANES = 128
VMEM_LIMIT = 56 * 1024 * 1024
ROW_CHUNK = 32

BF16 = jnp.bfloat16
F32 = jnp.float32


def _cparams(n_axes):
    return pltpu.CompilerParams(
        dimension_semantics=("arbitrary",) * n_axes, vmem_limit_bytes=VMEM_LIMIT)


def _dot_nt(a, b):
    return lax.dot_general(a, b, (((1,), (1,)), ((), ())), preferred_element_type=F32)


def _dot(a, b):
    return jnp.dot(a, b, preferred_element_type=F32)


def _split3(x):
    hi = x.astype(BF16)
    r1 = x - hi.astype(F32)
    mid = r1.astype(BF16)
    lo = (r1 - mid.astype(F32)).astype(BF16)
    return hi, mid, lo


def _head_norm_store(acc, gain_ref, out_ref, hd):
    tn = acc.shape[1]
    for c in range(tn // LANES):
        sl = slice(c * LANES, (c + 1) * LANES)
        y = acc[:, sl]
        sq = y * y
        if hd == LANES:
            ms = jnp.sum(sq, axis=-1, keepdims=True) * (1.0 / hd)
        else:
            lo = lax.broadcasted_iota(jnp.int32, y.shape, 1) < hd
            s_lo = jnp.sum(jnp.where(lo, sq, 0.0), axis=-1, keepdims=True)
            s_hi = jnp.sum(jnp.where(lo, 0.0, sq), axis=-1, keepdims=True)
            ms = jnp.where(lo, s_lo, s_hi) * (1.0 / hd)
        out_ref[:, sl] = y * lax.rsqrt(ms + EPS) * gain_ref[:, sl]


def _proj_in_kernel(*refs, q_tiles, k_tiles, hd, has_forget):
    if has_forget:
        (x_ref, g_ref, w_ref, qg_ref, kg_ref, wf_ref, bf_ref, u_ref, lf_ref, h_ref) = refs
    else:
        (x_ref, g_ref, w_ref, qg_ref, kg_ref, u_ref, h_ref) = refs
    j = pl.program_id(1)

    @pl.when(j == 0)
    def _():
        def norm_rows(n, carry):
            rows = pl.ds(pl.multiple_of(n * ROW_CHUNK, ROW_CHUNK), ROW_CHUNK)
            x = x_ref[rows, :]
            ms = jnp.mean(x * x, axis=-1, keepdims=True)
            h_ref[rows, :] = (x * lax.rsqrt(ms + EPS) * g_ref[...]).astype(BF16)
            return carry

        lax.fori_loop(0, x_ref.shape[0] // ROW_CHUNK, norm_rows, 0)
        if has_forget:
            fl = _dot(h_ref[...], wf_ref[...]) + bf_ref[...]
            lf_ref[...] = jnp.minimum(fl, 0.0) - jnp.log1p(jnp.exp(-jnp.abs(fl)))

    acc = _dot(h_ref[...], w_ref[...])
    is_q = (j >= q_tiles[0]) & (j < q_tiles[1])
    is_k = (j >= k_tiles[0]) & (j < k_tiles[1])

    @pl.when(is_q)
    def _():
        _head_norm_store(acc, qg_ref, u_ref, hd)

    @pl.when(is_k)
    def _():
        _head_norm_store(acc, kg_ref, u_ref, hd)

    @pl.when(jnp.logical_not(is_q | is_k))
    def _():
        u_ref[...] = acc


def _proj_in(x, g, w, qg, kg, *, tm, tn, q_tiles, k_tiles, hd, wf=None, bf=None):
    n, d = x.shape
    c = w.shape[1]
    has_forget = wf is not None
    in_specs = [
        pl.BlockSpec((tm, d), lambda i, j: (i, 0)),
        pl.BlockSpec((1, d), lambda i, j: (0, 0)),
        pl.BlockSpec((d, tn), lambda i, j: (0, j)),
        pl.BlockSpec((1, tn), lambda i, j: (0, 0)),
        pl.BlockSpec((1, tn), lambda i, j: (0, 0)),
    ]
    args = [x, g, w, qg, kg]
    out_shape = [jax.ShapeDtypeStruct((n, c), F32)]
    out_specs = [pl.BlockSpec((tm, tn), lambda i, j: (i, j))]
    if has_forget:
        in_specs += [pl.BlockSpec((d, FORGET_LANES), lambda i, j: (0, 0)),
                     pl.BlockSpec((1, FORGET_LANES), lambda i, j: (0, 0))]
        args += [wf, bf]
        out_shape.append(jax.ShapeDtypeStruct((n, FORGET_LANES), F32))
        out_specs.append(pl.BlockSpec((tm, FORGET_LANES), lambda i, j: (i, 0)))
    res = pl.pallas_call(
        functools.partial(_proj_in_kernel, q_tiles=q_tiles, k_tiles=k_tiles, hd=hd,
                          has_forget=has_forget),
        grid=(n // tm, c // tn),
        in_specs=in_specs,
        out_specs=out_specs,
        out_shape=out_shape,
        scratch_shapes=[pltpu.VMEM((tm, d), BF16)],
        compiler_params=_cparams(2),
        name="proj_in",
    )(*args)
    return res if has_forget else res[0]


def _proj_out_kernel(o_ref, z_ref, w_ref, x_ref, y_ref, gz_ref):
    @pl.when(pl.program_id(1) == 0)
    def _():
        def gate_rows(n, carry):
            rows = pl.ds(pl.multiple_of(n * ROW_CHUNK, ROW_CHUNK), ROW_CHUNK)
            z = z_ref[rows, :]
            silu = z * (1.0 / (1.0 + jnp.exp(-z)))
            gz_ref[rows, :] = (o_ref[rows, :] * silu).astype(BF16)
            return carry

        lax.fori_loop(0, o_ref.shape[0] // ROW_CHUNK, gate_rows, 0)

    y_ref[...] = x_ref[...] + _dot(gz_ref[...], w_ref[...])


def _proj_out(o, u, z_block, w, x, *, tm, tn):
    n, d = o.shape
    dm = w.shape[1]
    return pl.pallas_call(
        _proj_out_kernel,
        grid=(n // tm, dm // tn),
        in_specs=[
            pl.BlockSpec((tm, d), lambda i, j: (i, 0)),
            pl.BlockSpec((tm, d), lambda i, j: (i, z_block)),
            pl.BlockSpec((d, tn), lambda i, j: (0, j)),
            pl.BlockSpec((tm, tn), lambda i, j: (i, j)),
        ],
        out_specs=pl.BlockSpec((tm, tn), lambda i, j: (i, j)),
        out_shape=jax.ShapeDtypeStruct((n, dm), F32),
        scratch_shapes=[pltpu.VMEM((tm, d), BF16)],
        compiler_params=_cparams(2),
        name="proj_out",
    )(o, u, w, x)


def _cumsum_kernel(x_ref, f_ref, carry_ref):
    @pl.when(pl.program_id(1) == 0)
    def _():
        carry_ref[...] = jnp.zeros_like(carry_ref)

    x = x_ref[0]
    tb = x.shape[0]
    r = lax.broadcasted_iota(jnp.int32, (tb, tb), 0)
    c = lax.broadcasted_iota(jnp.int32, (tb, tb), 1)
    tri = jnp.where(c <= r, 1.0, 0.0).astype(BF16)
    hi, mid, lo = _split3(x)
    f = _dot(tri, hi) + _dot(tri, mid) + _dot(tri, lo) + carry_ref[...]
    f_ref[0] = f
    carry_ref[...] = f[tb - 1:tb, :]


def _cumsum_rows(x, tb):
    b, t, w = x.shape
    return pl.pallas_call(
        _cumsum_kernel,
        grid=(b, t // tb),
        in_specs=[pl.BlockSpec((1, tb, w), lambda i, j: (i, j, 0))],
        out_specs=pl.BlockSpec((1, tb, w), lambda i, j: (i, j, 0)),
        out_shape=jax.ShapeDtypeStruct((b, t, w), F32),
        scratch_shapes=[pltpu.VMEM((1, w), F32)],
        compiler_params=_cparams(2),
        name="cumsum_rows",
    )(x)


def _head_rows(f, heads):
    b, t, _ = f.shape
    return jnp.swapaxes(f[:, :, :heads], 1, 2).reshape(b * heads, 1, t)


def _fox_prompt_kernel(q_ref, k_ref, v_ref, f_ref, o_ref, kb_ref, vb_ref, m_ref, l_ref,
                       acc_ref, *, blk):
    qi = pl.program_id(2)

    @pl.when(qi == 0)
    def _():
        kb_ref[...] = k_ref[0].astype(BF16)
        vb_ref[...] = v_ref[0].astype(BF16)

    qb = (q_ref[0] * (A_DH ** -0.5)).astype(BF16)
    m_ref[...] = jnp.full_like(m_ref, NEG)
    l_ref[...] = jnp.zeros_like(l_ref)
    acc_ref[...] = jnp.zeros_like(acc_ref)

    def step(j, masked):
        off = pl.multiple_of(j * blk, blk)
        s = _dot_nt(qb, kb_ref[pl.ds(off, blk), :]) - f_ref[0, :, pl.ds(off, blk)]
        if masked:
            qpos = qi * blk + lax.broadcasted_iota(jnp.int32, (blk, blk), 0)
            kpos = j * blk + lax.broadcasted_iota(jnp.int32, (blk, blk), 1)
            s = jnp.where((kpos <= qpos) & (kpos >= LEAD_PAD), s, NEG)
        m_prev = m_ref[...]
        m_new = jnp.maximum(m_prev, jnp.max(s, axis=-1, keepdims=True))
        alpha = jnp.exp(m_prev - m_new)
        p = jnp.exp(s - m_new)
        l_ref[...] = alpha * l_ref[...] + jnp.sum(p, axis=-1, keepdims=True)
        acc_ref[...] = alpha * acc_ref[...] + _dot(p.astype(BF16), vb_ref[pl.ds(off, blk), :])
        m_ref[...] = m_new

    step(0, True)

    def body(j, carry):
        step(j, False)
        return carry

    lax.fori_loop(1, qi, body, 0)

    @pl.when(qi > 0)
    def _():
        step(qi, True)

    o_ref[0] = acc_ref[...] / l_ref[...]


def _fox_prompt(u, f_rows, batch, heads, blk):
    _, length, _ = u.shape
    return pl.pallas_call(
        functools.partial(_fox_prompt_kernel, blk=blk),
        grid=(batch, heads, length // blk),
        in_specs=[
            pl.BlockSpec((1, blk, A_DH), lambda b, h, i: (b, i, h)),
            pl.BlockSpec((1, length, A_DH), lambda b, h, i: (b, 0, heads + h)),
            pl.BlockSpec((1, length, A_DH), lambda b, h, i: (b, 0, 2 * heads + h)),
            pl.BlockSpec((1, 1, length), lambda b, h, i: (b * heads + h, 0, 0)),
        ],
        out_specs=pl.BlockSpec((1, blk, A_DH), lambda b, h, i: (b, i, h)),
        out_shape=jax.ShapeDtypeStruct((batch, length, heads * A_DH), F32),
        scratch_shapes=[
            pltpu.VMEM((length, A_DH), BF16),
            pltpu.VMEM((length, A_DH), BF16),
            pltpu.VMEM((blk, 1), F32),
            pltpu.VMEM((blk, 1), F32),
            pltpu.VMEM((blk, A_DH), F32),
        ],
        compiler_params=_cparams(3),
        name="fox_prompt",
    )(u, u, u, f_rows)


def _fox_sample_kernel(q_ref, kn_ref, vn_ref, ck_ref, cv_ref, f_ref, o_ref):
    t = q_ref.shape[1]
    past = ck_ref.shape[1]
    qb = (q_ref[0] * (A_DH ** -0.5)).astype(BF16)
    s_c = _dot_nt(qb, ck_ref[0].astype(BF16)) - f_ref[0, :, :past]
    s_n = _dot_nt(qb, kn_ref[0].astype(BF16)) - f_ref[0, :, past:]
    r = lax.broadcasted_iota(jnp.int32, (t, t), 0)
    c = lax.broadcasted_iota(jnp.int32, (t, t), 1)
    s_n = jnp.where(c <= r, s_n, NEG)
    m = jnp.maximum(jnp.max(s_c, axis=-1, keepdims=True), jnp.max(s_n, axis=-1, keepdims=True))
    p_c = jnp.exp(s_c - m)
    p_n = jnp.exp(s_n - m)
    l = jnp.sum(p_c, axis=-1, keepdims=True) + jnp.sum(p_n, axis=-1, keepdims=True)
    o = _dot(p_c.astype(BF16), cv_ref[0].astype(BF16)) + _dot(p_n.astype(BF16), vn_ref[0].astype(BF16))
    o_ref[0] = o / l


def _fox_sample(u, ck, cv, f_rows, heads):
    b, t, _ = u.shape
    past = ck.shape[1]
    return pl.pallas_call(
        _fox_sample_kernel,
        grid=(b, heads),
        in_specs=[
            pl.BlockSpec((1, t, A_DH), lambda i, h: (i, 0, h)),
            pl.BlockSpec((1, t, A_DH), lambda i, h: (i, 0, heads + h)),
            pl.BlockSpec((1, t, A_DH), lambda i, h: (i, 0, 2 * heads + h)),
            pl.BlockSpec((1, past, A_DH), lambda i, h: (i, 0, h)),
            pl.BlockSpec((1, past, A_DH), lambda i, h: (i, 0, h)),
            pl.BlockSpec((1, 1, past + t), lambda i, h: (i * heads + h, 0, 0)),
        ],
        out_specs=pl.BlockSpec((1, t, A_DH), lambda i, h: (i, 0, h)),
        out_shape=jax.ShapeDtypeStruct((b, t, heads * A_DH), F32),
        compiler_params=_cparams(2),
        name="fox_sample",
    )(u, u, u, ck, cv, f_rows)


def _sb_terms(z):
    log_b = jnp.minimum(z, 0.0) - jnp.log1p(jnp.exp(-jnp.abs(z)))
    return log_b, log_b - z


def _later_tri(n):
    j = lax.broadcasted_iota(jnp.int32, (n, n), 0)
    s = lax.broadcasted_iota(jnp.int32, (n, n), 1)
    return jnp.where(j > s, 1.0, 0.0).astype(BF16)


def _later_sums(x, tri):
    hi, mid, lo = _split3(x)
    return _dot(hi, tri) + _dot(mid, tri) + _dot(lo, tri)


def _sb_group(log_b, l1, mask, tri, later, vb):
    within = _later_sums(l1, tri)
    a = jnp.exp(log_b + within + later)
    if mask is not None:
        a = jnp.where(mask, a, 0.0)
    return _dot(a.astype(BF16), vb), later + within[:, 0:1] + l1[:, 0:1]


def _sb_prompt_kernel(q_ref, k_ref, v_ref, o_ref, kb_ref, vb_ref, later_ref, acc_ref, *, blk):
    qi = pl.program_id(2)

    @pl.when(qi == 0)
    def _():
        kb_ref[...] = k_ref[0].astype(BF16)
        vb_ref[...] = v_ref[0].astype(BF16)

    qb = (q_ref[0] * (C_DH ** -0.5)).astype(BF16)
    later_ref[...] = jnp.zeros_like(later_ref)
    acc_ref[...] = jnp.zeros_like(acc_ref)
    tri = _later_tri(LANES)

    def step(j, masked):
        off = pl.multiple_of(j * blk, blk)
        z = _dot_nt(qb, kb_ref[pl.ds(off, blk), :])
        later = later_ref[...]
        acc = acc_ref[...]
        for g in reversed(range(blk // LANES)):
            log_b, l1 = _sb_terms(z[:, g * LANES:(g + 1) * LANES])
            mask = None
            if masked:
                qpos = qi * blk + lax.broadcasted_iota(jnp.int32, (blk, LANES), 0)
                kpos = j * blk + g * LANES + lax.broadcasted_iota(jnp.int32, (blk, LANES), 1)
                mask = (kpos < qpos) & (kpos >= LEAD_PAD)
                l1 = jnp.where(mask, l1, 0.0)
            goff = pl.multiple_of(off + g * LANES, LANES)
            contrib, later = _sb_group(log_b, l1, mask, tri, later, vb_ref[pl.ds(goff, LANES), :])
            acc = acc + contrib
        later_ref[...] = later
        acc_ref[...] = acc

    @pl.when(qi > 0)
    def _():
        step(qi, True)

    def body(n, carry):
        step(qi - 1 - n, False)
        return carry

    lax.fori_loop(0, qi - 1, body, 0)
    step(0, True)
    o_ref[0] = acc_ref[...]


def _sb_prompt(u, batch, heads, blk):
    _, length, _ = u.shape
    return pl.pallas_call(
        functools.partial(_sb_prompt_kernel, blk=blk),
        grid=(batch, heads, length // blk),
        in_specs=[
            pl.BlockSpec((1, blk, C_DH), lambda b, h, i: (b, i, h)),
            pl.BlockSpec((1, length, C_DH), lambda b, h, i: (b, 0, heads + h)),
            pl.BlockSpec((1, length, C_DH), lambda b, h, i: (b, 0, 2 * heads + h)),
        ],
        out_specs=pl.BlockSpec((1, blk, C_DH), lambda b, h, i: (b, i, h)),
        out_shape=jax.ShapeDtypeStruct((batch, length, heads * C_DH), F32),
        scratch_shapes=[
            pltpu.VMEM((length, C_DH), BF16),
            pltpu.VMEM((length, C_DH), BF16),
            pltpu.VMEM((blk, 1), F32),
            pltpu.VMEM((blk, C_DH), F32),
        ],
        compiler_params=_cparams(3),
        name="sb_prompt",
    )(u, u, u)


def _sb_sample_kernel(q_ref, kn_ref, vn_ref, ck_ref, cv_ref, o_ref):
    t = q_ref.shape[1]
    past = ck_ref.shape[1]
    qb = (q_ref[0] * (C_DH ** -0.5)).astype(BF16)
    log_b, l1 = _sb_terms(_dot_nt(qb, kn_ref[0].astype(BF16)))
    r = lax.broadcasted_iota(jnp.int32, (t, t), 0)
    c = lax.broadcasted_iota(jnp.int32, (t, t), 1)
    mask = c < r
    l1 = jnp.where(mask, l1, 0.0)
    acc, later = _sb_group(log_b, l1, mask, _later_tri(t), jnp.zeros((t, 1), F32),
                           vn_ref[0].astype(BF16))
    tri = _later_tri(LANES)

    def body(n, carry):
        acc, later = carry
        off = pl.multiple_of(past - (n + 1) * LANES, LANES)
        kb = ck_ref[0, pl.ds(off, LANES), :].astype(BF16)
        vb = cv_ref[0, pl.ds(off, LANES), :].astype(BF16)
        log_b, l1 = _sb_terms(_dot_nt(qb, kb))
        contrib, later = _sb_group(log_b, l1, None, tri, later, vb)
        return acc + contrib, later

    acc, _ = lax.fori_loop(0, past // LANES, body, (acc, later))
    o_ref[0] = acc


def _sb_sample(u, ck, cv, heads):
    b, t, _ = u.shape
    past = ck.shape[1]
    return pl.pallas_call(
        _sb_sample_kernel,
        grid=(b, heads),
        in_specs=[
            pl.BlockSpec((1, t, C_DH), lambda i, h: (i, 0, h)),
            pl.BlockSpec((1, t, C_DH), lambda i, h: (i, 0, heads + h)),
            pl.BlockSpec((1, t, C_DH), lambda i, h: (i, 0, 2 * heads + h)),
            pl.BlockSpec((1, past, C_DH), lambda i, h: (i, 0, h)),
            pl.BlockSpec((1, past, C_DH), lambda i, h: (i, 0, h)),
        ],
        out_specs=pl.BlockSpec((1, t, C_DH), lambda i, h: (i, 0, h)),
        out_shape=jax.ShapeDtypeStruct((b, t, heads * C_DH), F32),
        compiler_params=_cparams(2),
        name="sb_sample",
    )(u, u, u, ck, cv)


def _alibi_slope(h, n_heads):
    return float(np.float32(2.0 ** (-8.0 * (h + 1) / n_heads)))


def _swa_heads(q_ref, o_ref, sinks_ref, kpieces, vpieces, dists, valids, n_heads):
    tq = q_ref.shape[1]
    low = lax.broadcasted_iota(jnp.int32, (tq, LANES), 1) < B_DH
    n_pairs = kpieces[0].shape[1] // LANES
    kb = [[kp[:, c * LANES:(c + 1) * LANES].astype(BF16) for c in range(n_pairs)] for kp in kpieces]
    vb = [[vp[:, c * LANES:(c + 1) * LANES].astype(BF16) for c in range(n_pairs)] for vp in vpieces]
    for pair in range(n_heads // 2):
        sl = slice(pair * LANES, (pair + 1) * LANES)
        qp = q_ref[0, :, sl] * (B_DH ** -0.5)
        qp_swapped = pltpu.roll(qp, B_DH, 1)
        outs = []
        for half in range(2):
            h = 2 * pair + half
            kv = h // B_GROUP
            kv_pair, kv_half = kv // 2, kv % 2
            src = qp if half == kv_half else qp_swapped
            keep = low if kv_half == 0 else jnp.logical_not(low)
            qh = jnp.where(keep, src, 0.0).astype(BF16)
            slope = _alibi_slope(h, n_heads)
            sink = sinks_ref[h]
            scores = []
            m = jnp.full((tq, 1), sink, F32)
            for kbp, dist, valid in zip(kb, dists, valids):
                s = _dot_nt(qh, kbp[kv_pair]) - slope * dist
                if valid is not None:
                    s = jnp.where(valid, s, NEG)
                scores.append(s)
                m = jnp.maximum(m, jnp.max(s, axis=-1, keepdims=True))
            den = jnp.exp(sink - m)
            o = jnp.zeros((tq, LANES), F32)
            for s, vbp in zip(scores, vb):
                p = jnp.exp(s - m)
                den = den + jnp.sum(p, axis=-1, keepdims=True)
                o = o + _dot(p.astype(BF16), vbp[kv_pair])
            o = o / den
            outs.append(o if half == kv_half else pltpu.roll(o, B_DH, 1))
        o_ref[0, :, sl] = jnp.where(low, outs[0], outs[1])


def _swa_prompt_kernel(sinks_ref, q_ref, kvp_ref, kvc_ref, o_ref, *, n_heads):
    i = pl.program_id(1)
    tq = q_ref.shape[1]
    kv_w = kvp_ref.shape[2] // 2
    r = lax.broadcasted_iota(jnp.int32, (tq, tq), 0)
    c = lax.broadcasted_iota(jnp.int32, (tq, tq), 1)
    q_chunk = r // CHUNK
    k_chunk = c // CHUNK
    valid_prev = (k_chunk >= q_chunk) & ((i - 1) * tq + c >= LEAD_PAD)
    valid_cur = (k_chunk <= q_chunk) & (i * tq + c >= LEAD_PAD)
    dist_prev = jnp.abs(tq + r - c).astype(F32)
    dist_cur = jnp.abs(r - c).astype(F32)
    kvp = kvp_ref[0]
    kvc = kvc_ref[0]
    _swa_heads(q_ref, o_ref, sinks_ref,
               [kvp[:, :kv_w], kvc[:, :kv_w]], [kvp[:, kv_w:], kvc[:, kv_w:]],
               [dist_prev, dist_cur], [valid_prev, valid_cur], n_heads)


def _swa_prompt(u, sinks, n_heads, kv_block):
    batch, length, _ = u.shape
    d = n_heads * B_DH
    kv_w = 2 * (n_heads // B_GROUP) * B_DH
    tq = WINDOW
    return pl.pallas_call(
        functools.partial(_swa_prompt_kernel, n_heads=n_heads),
        grid=(batch, length // tq),
        in_specs=[
            pl.BlockSpec(memory_space=pltpu.SMEM),
            pl.BlockSpec((1, tq, d), lambda b, i: (b, i, 0)),
            pl.BlockSpec((1, tq, kv_w), lambda b, i: (b, jnp.maximum(i - 1, 0), kv_block)),
            pl.BlockSpec((1, tq, kv_w), lambda b, i: (b, i, kv_block)),
        ],
        out_specs=pl.BlockSpec((1, tq, d), lambda b, i: (b, i, 0)),
        out_shape=jax.ShapeDtypeStruct((batch, length, d), F32),
        compiler_params=_cparams(2),
        name="swa_prompt",
    )(sinks, u, u, u)


def _swa_sample_kernel(sinks_ref, q_ref, kvn_ref, ck_ref, cv_ref, o_ref, *, n_heads):
    t = q_ref.shape[1]
    rows = ck_ref.shape[1]
    kv_w = kvn_ref.shape[2] // 2
    r_c = lax.broadcasted_iota(jnp.int32, (t, rows), 0)
    c_c = lax.broadcasted_iota(jnp.int32, (t, rows), 1)
    r_n = lax.broadcasted_iota(jnp.int32, (t, t), 0)
    c_n = lax.broadcasted_iota(jnp.int32, (t, t), 1)
    dist_c = jnp.abs(rows + r_c - c_c).astype(F32)
    dist_n = jnp.abs(r_n - c_n).astype(F32)
    kvn = kvn_ref[0]
    _swa_heads(q_ref, o_ref, sinks_ref,
               [ck_ref[0], kvn[:, :kv_w]], [cv_ref[0], kvn[:, kv_w:]],
               [dist_c, dist_n], [None, None], n_heads)


def _swa_sample(u, ck, cv, sinks, n_heads, kv_block):
    b, t, _ = u.shape
    rows = ck.shape[1]
    d = n_heads * B_DH
    kv_w = 2 * (n_heads // B_GROUP) * B_DH
    return pl.pallas_call(
        functools.partial(_swa_sample_kernel, n_heads=n_heads),
        grid=(b,),
        in_specs=[
            pl.BlockSpec(memory_space=pltpu.SMEM),
            pl.BlockSpec((1, t, d), lambda i: (i, 0, 0)),
            pl.BlockSpec((1, t, kv_w), lambda i: (i, 0, kv_block)),
            pl.BlockSpec((1, rows, kv_w // 2), lambda i: (i, 0, 0)),
            pl.BlockSpec((1, rows, kv_w // 2), lambda i: (i, 0, 0)),
        ],
        out_specs=pl.BlockSpec((1, t, d), lambda i: (i, 0, 0)),
        out_shape=jax.ShapeDtypeStruct((b, t, d), F32),
        compiler_params=_cparams(1),
        name="swa_sample",
    )(sinks, u, u, ck, cv)


def _tile_gain(g, tn):
    return jnp.tile(g.astype(F32), tn // g.shape[0]).reshape(1, tn)


def kernel(x_prompt, x_sample, cache_k_a, cache_v_a, cache_logf_a, cache_k_b, cache_v_b, cache_k_c, cache_v_c, meta_tokens, norm_g, w_in_a, w_out_a, b_f_a, qn_g_a, kn_g_a, w_in_b, w_out_b, sinks_b, qn_g_b, kn_g_b, w_in_c, w_out_c):
    batch, seq, d = x_prompt.shape
    dec_b, dec_t, _ = x_sample.shape
    depth = norm_g.shape[0]
    length = seq + Q_BLOCK
    a_heads = d // A_DH
    b_heads = d // B_DH
    b_kvw = (b_heads // B_GROUP) * B_DH
    c_heads = d // C_DH
    past = cache_k_a.shape[2]
    dt = x_prompt.dtype

    tm_in, tm_out, attn_blk = 1280, 640, 640
    n_p, n_s = batch * length, dec_b * dec_t

    xp = jnp.concatenate([jnp.zeros((batch, LEAD_PAD, d), dt),
                          jnp.broadcast_to(meta_tokens.astype(dt), (batch, N_META, d)),
                          x_prompt], axis=1).reshape(n_p, d)
    xs = x_sample.reshape(n_s, d)

    outs = {k: [] for k in ("ka_p", "va_p", "fa_p", "ka_s", "va_s", "fa_s", "kb_p", "vb_p",
                            "kb_s", "vb_s", "kc_p", "vc_p", "kc_s", "vc_s")}
    for i in range(depth):
        j = i // N_MIXERS
        g = norm_g[i].astype(F32).reshape(1, d)
        if i % N_MIXERS == 0:
            tn = 512
            w = w_in_a[j][:, :4 * d].astype(BF16)
            wf = jnp.pad(w_in_a[j][:, 4 * d:], ((0, 0), (0, FORGET_LANES - a_heads))).astype(BF16)
            bf = jnp.pad(b_f_a[j].astype(F32), (0, FORGET_LANES - a_heads)).reshape(1, FORGET_LANES)
            qg, kg = _tile_gain(qn_g_a[j], tn), _tile_gain(kn_g_a[j], tn)
            kw = dict(tn=tn, q_tiles=(0, d // tn), k_tiles=(d // tn, 2 * d // tn), hd=A_DH, wf=wf, bf=bf)
            up, lfp = _proj_in(xp, g, w, qg, kg, tm=tm_in, **kw)
            us, lfs = _proj_in(xs, g, w, qg, kg, tm=n_s, **kw)
            up3 = up.reshape(batch, length, 4 * d)
            us3 = us.reshape(dec_b, dec_t, 4 * d)
            lfp3 = lfp.reshape(batch, length, FORGET_LANES)
            lfs3 = lfs.reshape(dec_b, dec_t, FORGET_LANES)
            f_p = _head_rows(_cumsum_rows(lfp3, attn_blk), a_heads)
            clf = jnp.pad(cache_logf_a[j].astype(F32), ((0, 0), (0, 0), (0, FORGET_LANES - a_heads)))
            f_s = _head_rows(_cumsum_rows(jnp.concatenate([clf, lfs3], axis=1), (past + dec_t) // 3), a_heads)
            op = _fox_prompt(up3, f_p, batch, a_heads, attn_blk)
            os_ = _fox_sample(us3, cache_k_a[j].reshape(dec_b, past, d).astype(F32),
                              cache_v_a[j].reshape(dec_b, past, d).astype(F32), f_s, a_heads)
            outs["ka_p"].append(up3[:, LEAD_PAD:, d:2 * d].reshape(batch, -1, a_heads, A_DH))
            outs["va_p"].append(up3[:, LEAD_PAD:, 2 * d:3 * d].reshape(batch, -1, a_heads, A_DH))
            outs["fa_p"].append(lfp3[:, LEAD_PAD:, :a_heads])
            outs["ka_s"].append(us3[:, :, d:2 * d].reshape(dec_b, dec_t, a_heads, A_DH))
            outs["va_s"].append(us3[:, :, 2 * d:3 * d].reshape(dec_b, dec_t, a_heads, A_DH))
            outs["fa_s"].append(lfs3[:, :, :a_heads])
            z_block, w_out = 3, w_out_a[j]
        elif i % N_MIXERS == 1:
            tn = 256
            wi = w_in_b[j]
            w = jnp.concatenate([wi[:, :d], wi[:, d + 2 * b_kvw:], wi[:, d:d + 2 * b_kvw]], axis=1).astype(BF16)
            qg, kg = _tile_gain(qn_g_b[j], tn), _tile_gain(kn_g_b[j], tn)
            kw = dict(tn=tn, q_tiles=(0, d // tn), k_tiles=(2 * d // tn, 2 * d // tn + b_kvw // tn), hd=B_DH)
            up = _proj_in(xp, g, w, qg, kg, tm=tm_in, **kw)
            us = _proj_in(xs, g, w, qg, kg, tm=n_s, **kw)
            cols = 2 * d + 2 * b_kvw
            up3 = up.reshape(batch, length, cols)
            us3 = us.reshape(dec_b, dec_t, cols)
            sinks = sinks_b[j].astype(F32)
            kv_block = 2 * d // (2 * b_kvw)
            rows = cache_k_b.shape[2]
            op = _swa_prompt(up3, sinks, b_heads, kv_block)
            os_ = _swa_sample(us3, cache_k_b[j].reshape(dec_b, rows, b_kvw).astype(F32),
                              cache_v_b[j].reshape(dec_b, rows, b_kvw).astype(F32), sinks, b_heads, kv_block)
            kshape = (b_heads // B_GROUP, B_DH)
            outs["kb_p"].append(up3[:, -rows:, 2 * d:2 * d + b_kvw].reshape((batch, rows) + kshape))
            outs["vb_p"].append(up3[:, -rows:, 2 * d + b_kvw:].reshape((batch, rows) + kshape))
            outs["kb_s"].append(us3[:, :, 2 * d:2 * d + b_kvw].reshape((dec_b, dec_t) + kshape))
            outs["vb_s"].append(us3[:, :, 2 * d + b_kvw:].reshape((dec_b, dec_t) + kshape))
            z_block, w_out = 1, w_out_b[j]
        else:
            tn = 512
            w = w_in_c[j].astype(BF16)
            ones = jnp.ones((1, tn), F32)
            kw = dict(tn=tn, q_tiles=(0, 0), k_tiles=(0, 0), hd=C_DH)
            up = _proj_in(xp, g, w, ones, ones, tm=tm_in, **kw)
            us = _proj_in(xs, g, w, ones, ones, tm=n_s, **kw)
            up3 = up.reshape(batch, length, 4 * d)
            us3 = us.reshape(dec_b, dec_t, 4 * d)
            op = _sb_prompt(up3, batch, c_heads, attn_blk)
            os_ = _sb_sample(us3, cache_k_c[j].reshape(dec_b, past, d).astype(F32),
                             cache_v_c[j].reshape(dec_b, past, d).astype(F32), c_heads)
            outs["kc_p"].append(up3[:, LEAD_PAD:, d:2 * d].reshape(batch, -1, c_heads, C_DH))
            outs["vc_p"].append(up3[:, LEAD_PAD:, 2 * d:3 * d].reshape(batch, -1, c_heads, C_DH))
            outs["kc_s"].append(us3[:, :, d:2 * d].reshape(dec_b, dec_t, c_heads, C_DH))
            outs["vc_s"].append(us3[:, :, 2 * d:3 * d].reshape(dec_b, dec_t, c_heads, C_DH))
            z_block, w_out = 3, w_out_c[j]
        w_out = w_out.astype(BF16)
        xp = _proj_out(op.reshape(n_p, d), up, z_block, w_out, xp, tm=tm_out, tn=1024)
        xs = _proj_out(os_.reshape(n_s, d), us, z_block, w_out, xs, tm=n_s, tn=1024)

    y_prompt = xp.reshape(batch, length, d)[:, Q_BLOCK:]
    y_sample = xs.reshape(dec_b, dec_t, d)
    st = {k: jnp.stack(v) for k, v in outs.items()}
    return (y_prompt, y_sample,
            st["ka_p"], st["va_p"], st["fa_p"], st["ka_s"], st["va_s"], st["fa_s"],
            st["kb_p"], st["vb_p"], st["kb_s"], st["vb_s"],
            st["kc_p"], st["vc_p"], st["kc_s"], st["vc_s"])
```
